```python
import jax, jax.numpy as jnp
from jax import lax
import numpy as np

D_MODEL = 1024
BATCH = 4
SEQ = 4096
DEPTH = 4

RET_HEADS = 4
RET_DK = 64
RET_DV = 128
RET_CHUNK = 128
ROPE_BASE = 10000.0
POOL_GROUPS = 4
POOL_DIM = 128
POOL_WINDOWS = (2, 4, 8, 16)
GDN_HEADS = 4
GDN_DK = 128
GDN_DV = 128
GDN_CONV = 4
GDN_CHUNK = 64
BRANCH_W = 512
N_BRANCH = 3
MOE_GROUPS = 4
MOE_EXPERTS_PER_GROUP = 4
MOE_TOP_K = 2
MOE_HIDDEN = 512
N_EXPERTS = MOE_GROUPS * MOE_EXPERTS_PER_GROUP
DEEPNORM_ALPHA = (2 * DEPTH) ** 0.25
DEEPNORM_BETA = (8 * DEPTH) ** -0.25
LN_EPS = 1e-5
RMS_EPS = 1e-6
IN_SIZES = (RET_HEADS * RET_DK, RET_HEADS * RET_DK, RET_HEADS * RET_DV, RET_HEADS * RET_DV,
            POOL_GROUPS * POOL_DIM,
            GDN_HEADS * GDN_DK, GDN_HEADS * GDN_DK, GDN_HEADS * GDN_DV, GDN_HEADS * GDN_DV,
            GDN_HEADS, GDN_HEADS,
            N_BRANCH * D_MODEL)
IN_COLS = sum(IN_SIZES)

kernel_name = "hybrid_retention_pool_gdn_hiermoe_deepnorm"


def layer_norm(x, g, b, dtype):
    xf = x.astype(jnp.float32)
    mu = jnp.mean(xf, axis=-1, keepdims=True)
    xc = xf - mu
    var = jnp.mean(xc * xc, axis=-1, keepdims=True)
    return (xc * lax.rsqrt(var + LN_EPS) * g.astype(jnp.float32) + b.astype(jnp.float32)).astype(dtype)


def rotary(x, pos):
    half = x.shape[-1] // 2
    inv_freq = ROPE_BASE ** (-jnp.arange(half, dtype=jnp.float32) / half)
    ang = pos[:, None] * inv_freq[None, :]
    cos = jnp.cos(ang)[None, :, None, :]
    sin = jnp.sin(ang)[None, :, None, :]
    x1, x2 = x[..., :half], x[..., half:]
    return jnp.concatenate([x1 * cos - x2 * sin, x1 * sin + x2 * cos], axis=-1)


def chunk_retention(q, k, v, log_gamma):
    B, T, H, dk = q.shape
    dv = v.shape[-1]
    C = RET_CHUNK
    N = T // C
    q = q.reshape(B, N, C, H, dk)
    k = k.reshape(B, N, C, H, dk)
    v = v.reshape(B, N, C, H, dv)
    idx = jnp.arange(C, dtype=jnp.float32)
    rel = idx[:, None] - idx[None, :]
    decay = jnp.where(rel[None] >= 0, jnp.exp(jnp.maximum(rel, 0.0)[None] * log_gamma[:, None, None]), 0.0)
    scores = jnp.einsum('bnihd,bnjhd->bnhij', q, k) * decay[None, None]
    inner = jnp.einsum('bnhij,bnjhe->bnihe', scores, v)
    zeta = jnp.exp((C - 1 - idx)[:, None] * log_gamma[None, :])
    kv = jnp.einsum('bnjhd,bnjhe->nbhde', k * zeta[:, :, None], v)
    chunk_decay = jnp.exp(C * log_gamma)[:, None, None]

    def step(S, kv_n):
        return S * chunk_decay + kv_n, S

    _, S_prev = lax.scan(step, jnp.zeros((B, H, dk, dv), jnp.float32), kv)
    xi = jnp.exp((idx + 1.0)[:, None] * log_gamma[None, :])
    cross = jnp.einsum('bnihd,nbhde->bnihe', q * xi[:, :, None], S_prev)
    return (inner + cross).reshape(B, T, H, dv)


def retention_branch(q, k, v, gate, pos, log_gamma):
    B, T, _ = q.shape
    f32 = jnp.float32
    q = rotary(q.astype(f32).reshape(B, T, RET_HEADS, RET_DK), pos)
    k = rotary(k.astype(f32).reshape(B, T, RET_HEADS, RET_DK), pos) * (RET_DK ** -0.5)
    v = v.astype(f32).reshape(B, T, RET_HEADS, RET_DV)
    o = chunk_retention(q, k, v, log_gamma)
    mu = jnp.mean(o, axis=-1, keepdims=True)
    oc = o - mu
    o = oc * lax.rsqrt(jnp.mean(oc * oc, axis=-1, keepdims=True) + LN_EPS)
    return jax.nn.silu(gate.astype(f32)) * o.reshape(B, T, RET_HEADS * RET_DV)


def pool_branch(u, pool_w, pool_scale):
    B, T, _ = u.shape
    u = u.astype(jnp.float32).reshape(B, T, POOL_GROUPS, POOL_DIM)
    cs = jnp.concatenate([jnp.zeros((B, 1, POOL_GROUPS, POOL_DIM), jnp.float32), jnp.cumsum(u, axis=1)], axis=1)
    t = jnp.arange(T)
    outs = []
    for gi, w in enumerate(POOL_WINDOWS):
        lo = jnp.maximum(t + 1 - w, 0)
        s = cs[:, 1:, gi] - cs[:, lo, gi]
        cnt = (t + 1 - lo).astype(jnp.float32)
        outs.append(s / cnt[None, :, None] - u[:, :, gi])
    p = jnp.stack(outs, axis=2)
    y = jnp.einsum('btgc,gcd->btgd', p, pool_w.astype(jnp.float32))
    return y.reshape(B, T, POOL_GROUPS * POOL_DIM) * pool_scale.astype(jnp.float32)


def causal_dwconv(x, w):
    K = w.shape[0]
    return lax.conv_general_dilated(x, w[:, None, :], window_strides=(1,), padding=[(K - 1, 0)],
                                    dimension_numbers=('NWC', 'WIO', 'NWC'), feature_group_count=x.shape[-1])


def l2norm(x):
    return x * lax.rsqrt(jnp.sum(x * x, axis=-1, keepdims=True) + RMS_EPS)


def gated_delta_rule(q, k, v, g, beta):
    B, T, H, dk = q.shape
    dv = v.shape[-1]
    C = GDN_CHUNK
    N = T // C
    q = l2norm(q) * (dk ** -0.5)
    k = l2norm(k)

    def chunks(a):
        return a.reshape(B, N, C, H, -1).transpose(0, 3, 1, 2, 4)

    q, k, v = chunks(q), chunks(k), chunks(v)
    beta = beta.reshape(B, N, C, H).transpose(0, 3, 1, 2)
    g = jnp.cumsum(g.reshape(B, N, C, H).transpose(0, 3, 1, 2), axis=-1)
    idx = jnp.arange(C)
    incl = idx[:, None] >= idx[None, :]
    strict = idx[:, None] > idx[None, :]
    diff = g[..., :, None] - g[..., None, :]
    decay = jnp.where(incl, jnp.exp(jnp.where(incl, diff, 0.0)), 0.0)
    k_beta = k * beta[..., None]
    A = jnp.where(strict, jnp.einsum('bhnid,bhnjd->bhnij', k_beta, k) * decay, 0.0)
    L = A + jnp.eye(C, dtype=jnp.float32)
    rhs = jnp.concatenate([v * beta[..., None], k_beta * jnp.exp(g)[..., None]], axis=-1)
    sol = lax.linalg.triangular_solve(L, rhs, left_side=True, lower=True, unit_diagonal=True)
    u, w = sol[..., :dv], sol[..., dv:]
    attn = jnp.einsum('bhnid,bhnjd->bhnij', q, k) * decay
    g_last = g[..., -1]
    k_dec = k * jnp.exp(g_last[..., None] - g)[..., None]
    q_dec = q * jnp.exp(g)[..., None]
    xs = (jnp.moveaxis(q_dec, 2, 0), jnp.moveaxis(k_dec, 2, 0), jnp.moveaxis(u, 2, 0),
          jnp.moveaxis(w, 2, 0), jnp.moveaxis(attn, 2, 0), jnp.moveaxis(g_last, 2, 0))

    def step(S, inp):
        q_n, k_n, u_n, w_n, attn_n, gl_n = inp
        v_new = u_n - jnp.einsum('bhcd,bhde->bhce', w_n, S)
        o = jnp.einsum('bhcd,bhde->bhce', q_n, S) + jnp.einsum('bhij,bhje->bhie', attn_n, v_new)
        S = S * jnp.exp(gl_n)[..., None, None] + jnp.einsum('bhcd,bhce->bhde', k_n, v_new)
        return S, o

    _, o = lax.scan(step, jnp.zeros((B, H, dk, dv), jnp.float32), xs)
    return o.transpose(1, 0, 3, 2, 4).reshape(B, T, H, dv)


def gdn_branch(q, k, v, z, b_logit, a_logit, conv_w, A_log, dt_bias, norm_w):
    B, T, _ = q.shape
    f32 = jnp.float32
    qkv = jnp.concatenate([q, k, v], axis=-1).astype(f32)
    qkv = jax.nn.silu(causal_dwconv(qkv, conv_w.astype(f32)))
    q, k, v = jnp.split(qkv, [GDN_HEADS * GDN_DK, 2 * GDN_HEADS * GDN_DK], axis=-1)
    q = q.reshape(B, T, GDN_HEADS, GDN_DK)
    k = k.reshape(B, T, GDN_HEADS, GDN_DK)
    v = v.reshape(B, T, GDN_HEADS, GDN_DV)
    beta = jax.nn.sigmoid(b_logit.astype(f32))
    g = -jnp.exp(A_log.astype(f32)) * jax.nn.softplus(a_logit.astype(f32) + dt_bias.astype(f32))
    o = gated_delta_rule(q, k, v, g, beta)
    o = o * lax.rsqrt(jnp.mean(o * o, axis=-1, keepdims=True) + RMS_EPS) * norm_w.astype(f32)
    o = o * jax.nn.silu(z.astype(f32).reshape(B, T, GDN_HEADS, GDN_DV))
    return o.reshape(B, T, GDN_HEADS * GDN_DV)


def hier_moe(h, wc, bc, wf, bf, w_gate, w_up, w_down):
    B, T, D = h.shape
    f32 = jnp.float32
    xt = h.reshape(B * T, D)
    pc = jax.nn.softmax((xt @ wc).astype(f32) + bc.astype(f32), axis=-1)
    p_g, g_sel = lax.top_k(pc, 1)
    fine = ((xt @ wf).astype(f32) + bf.astype(f32)).reshape(-1, MOE_GROUPS, MOE_EXPERTS_PER_GROUP)
    fine_sel = jnp.take_along_axis(fine, g_sel[:, :, None], axis=1)[:, 0]
    pf = jax.nn.softmax(fine_sel, axis=-1)
    top_w, top_i = lax.top_k(pf, MOE_TOP_K)
    top_w = top_w / jnp.sum(top_w, axis=-1, keepdims=True)
    expert_id = g_sel * MOE_EXPERTS_PER_GROUP + top_i
    weights = p_g * top_w
    combine = jnp.einsum('nk,nke->ne', weights, jax.nn.one_hot(expert_id, N_EXPERTS, dtype=f32))
    out = jnp.zeros((B * T, D), f32)
    for e in range(N_EXPERTS):
        hdn = jax.nn.silu(xt @ w_gate[e]) * (xt @ w_up[e])
        out = out + combine[:, e:e + 1] * (hdn @ w_down[e])
    return out.reshape(B, T, D)


def setup_inputs(seed: int = 0) -> dict:
    key = jax.random.key(seed)
    ks = jax.random.split(key, 24)
    f32 = jnp.float32

    def nrm(k, shape, scale):
        return jax.random.normal(k, shape, f32) * scale

    b = DEEPNORM_BETA
    col_scale = np.concatenate([
        np.full(IN_SIZES[0] + IN_SIZES[1], 1.0), np.full(IN_SIZES[2], b), np.full(IN_SIZES[3], 1.0),
        np.full(IN_SIZES[4], b),
        np.full(IN_SIZES[5] + IN_SIZES[6], 1.0), np.full(IN_SIZES[7], b),
        np.full(IN_SIZES[8] + IN_SIZES[9] + IN_SIZES[10] + IN_SIZES[11], 1.0)]).astype(np.float32)
    x = jax.random.normal(ks[0], (BATCH, SEQ, D_MODEL), f32)
    w_in = nrm(ks[1], (DEPTH, D_MODEL, IN_COLS), D_MODEL ** -0.5) * jnp.asarray(col_scale)
    pool_w = nrm(ks[2], (DEPTH, POOL_GROUPS, POOL_DIM, POOL_DIM), POOL_DIM ** -0.5)
    pool_scale = 1.0 + nrm(ks[3], (DEPTH, POOL_GROUPS * POOL_DIM), 0.1)
    conv_w = nrm(ks[4], (DEPTH, GDN_CONV, 2 * GDN_HEADS * GDN_DK + GDN_HEADS * GDN_DV), GDN_CONV ** -0.5)
    A_log = jnp.log(jax.random.uniform(ks[5], (DEPTH, GDN_HEADS), f32, 1.0, 16.0))
    dt = jnp.exp(jax.random.uniform(ks[6], (DEPTH, GDN_HEADS), f32, np.log(1e-3), np.log(1e-1)))
    dt_bias = dt + jnp.log(-jnp.expm1(-dt))
    gdn_norm_w = 1.0 + nrm(ks[7], (DEPTH, GDN_DV), 0.02)
    w_branch = nrm(ks[8], (DEPTH, N_BRANCH, BRANCH_W, D_MODEL), BRANCH_W ** -0.5 * b)
    w_out = nrm(ks[9], (DEPTH, D_MODEL, D_MODEL), D_MODEL ** -0.5 * b)
    ln1_g = 1.0 + nrm(ks[10], (DEPTH, D_MODEL), 0.02)
    ln1_b = nrm(ks[11], (DEPTH, D_MODEL), 0.02)
    router_coarse_w = nrm(ks[12], (DEPTH, D_MODEL, MOE_GROUPS), D_MODEL ** -0.5)
    router_coarse_b = nrm(ks[13], (DEPTH, MOE_GROUPS), 0.01)
    router_fine_w = nrm(ks[14], (DEPTH, D_MODEL, N_EXPERTS), D_MODEL ** -0.5)
    router_fine_b = nrm(ks[15], (DEPTH, N_EXPERTS), 0.01)
    w_gate = nrm(ks[16], (DEPTH, N_EXPERTS, D_MODEL, MOE_HIDDEN), D_MODEL ** -0.5 * b)
    w_up = nrm(ks[17], (DEPTH, N_EXPERTS, D_MODEL, MOE_HIDDEN), D_MODEL ** -0.5 * b)
    w_down = nrm(ks[18], (DEPTH, N_EXPERTS, MOE_HIDDEN, D_MODEL), MOE_HIDDEN ** -0.5 * b)
    ln2_g = 1.0 + nrm(ks[19], (DEPTH, D_MODEL), 0.02)
    ln2_b = nrm(ks[20], (DEPTH, D_MODEL), 0.02)
    return {"x": x, "w_in": w_in, "pool_w": pool_w, "pool_scale": pool_scale, "conv_w": conv_w,
            "A_log": A_log, "dt_bias": dt_bias, "gdn_norm_w": gdn_norm_w, "w_branch": w_branch,
            "w_out": w_out, "ln1_g": ln1_g, "ln1_b": ln1_b, "router_coarse_w": router_coarse_w,
            "router_coarse_b": router_coarse_b, "router_fine_w": router_fine_w,
            "router_fine_b": router_fine_b, "w_gate": w_gate, "w_up": w_up, "w_down": w_down,
            "ln2_g": ln2_g, "ln2_b": ln2_b}


def reference(x, w_in, pool_w, pool_scale, conv_w, A_log, dt_bias, gdn_norm_w, w_branch, w_out,
              ln1_g, ln1_b, router_coarse_w, router_coarse_b, router_fine_w, router_fine_b,
              w_gate, w_up, w_down, ln2_g, ln2_b):
    dtype = x.dtype
    B, T, D = x.shape
    offsets = np.cumsum(IN_SIZES)[:-1].tolist()
    pos = jnp.arange(T, dtype=jnp.float32)
    log_gamma = jnp.log(1.0 - jnp.exp2(-5.0 - jnp.arange(RET_HEADS, dtype=jnp.float32)))
    for l in range(DEPTH):
        proj = jnp.einsum('btd,dc->btc', x, w_in[l])
        rq, rk, rv, rg, pu, gq, gk, gv, gz, gb, ga, gate_logits = jnp.split(proj, offsets, axis=-1)
        y_ret = retention_branch(rq, rk, rv, rg, pos, log_gamma)
        y_pool = pool_branch(pu, pool_w[l], pool_scale[l])
        y_gdn = gdn_branch(gq, gk, gv, gz, gb, ga, conv_w[l], A_log[l], dt_bias[l], gdn_norm_w[l])
        ys = jnp.stack([y_ret, y_pool, y_gdn], axis=2)
        up = jnp.einsum('btrc,rcd->btrd', ys, w_branch[l])
        gates = jax.nn.sigmoid(gate_logits.astype(jnp.float32).reshape(B, T, N_BRANCH, D))
        mixed = jnp.sum(gates * up, axis=2)
        mix_out = mixed @ w_out[l]
        x = layer_norm(DEEPNORM_ALPHA * x + mix_out, ln1_g[l], ln1_b[l], dtype)
        ffn_out = hier_moe(x, router_coarse_w[l], router_coarse_b[l], router_fine_w[l], router_fine_b[l],
                           w_gate[l], w_up[l], w_down[l])
        x = layer_norm(DEEPNORM_ALPHA * x + ffn_out, ln2_g[l], ln2_b[l], dtype)
    return x
```

```python
import functools
import math

import jax
import jax.numpy as jnp
from jax import lax
from jax.experimental import pallas as pl
from jax.experimental.pallas import tpu as pltpu

F32 = jnp.float32
BF16 = jnp.bfloat16

D_MODEL = 1024
DEPTH = 4
RET_HEADS, RET_DK, RET_DV = 4, 64, 128
ROPE_BASE = 10000.0
POOL_GROUPS, POOL_DIM = 4, 128
POOL_WINDOWS = (2, 4, 8, 16)
GDN_HEADS, GDN_DK, GDN_DV, GDN_CONV = 4, 128, 128, 4
BRANCH_W, N_BRANCH = 512, 3
MOE_GROUPS, MOE_EPG, MOE_HIDDEN = 4, 4, 512
N_EXPERTS = MOE_GROUPS * MOE_EPG
ALPHA = (2 * DEPTH) ** 0.25
LN_EPS = 1e-5
RMS_EPS = 1e-6
RET_LOG_GAMMA = tuple(math.log(1.0 - 2.0 ** (-5.0 - h)) for h in range(RET_HEADS))

_HK = RET_HEADS * RET_DK
_HV = RET_HEADS * RET_DV
RET_COLS = (0, 2 * _HK + 2 * _HV)
POOL_COLS = (RET_COLS[1], RET_COLS[1] + POOL_GROUPS * POOL_DIM)
GDN_COLS = (POOL_COLS[1], POOL_COLS[1] + 4 * GDN_HEADS * GDN_DK)
BA_COLS = (GDN_COLS[1], GDN_COLS[1] + 2 * GDN_HEADS)
GATE_COLS = (BA_COLS[1], BA_COLS[1] + N_BRANCH * D_MODEL)

LANES = 128
SEQ_TILE = 256
HALO = 16
CONV_HALO = 8
VMEM_LIMIT = 56 * 1024 * 1024


def _cparams(sem):
    return pltpu.CompilerParams(dimension_semantics=sem, vmem_limit_bytes=VMEM_LIMIT)


def _dot(a, b):
    return jnp.dot(a.astype(BF16), b.astype(BF16), preferred_element_type=F32)


def _dot_nt(a, b):
    return lax.dot_general(a.astype(BF16), b.astype(BF16), (((1,), (1,)), ((), ())),
                           preferred_element_type=F32)


def _iota(shape, dim):
    return lax.broadcasted_iota(jnp.int32, shape, dim)


def _layer_norm(h, g, b):
    mu = jnp.mean(h, axis=-1, keepdims=True)
    hc = h - mu
    var = jnp.mean(hc * hc, axis=-1, keepdims=True)
    return hc * lax.rsqrt(var + LN_EPS) * g + b


def _ret_body(x_ref, w_ref, cos_ref, sin_ref, y_ref, s_ref):
    c = x_ref.shape[1]

    @pl.when(pl.program_id(1) == 0)
    def _():
        s_ref[...] = jnp.zeros(s_ref.shape, F32)

    p = _dot(x_ref[0], w_ref[...])
    q, k = p[:, :_HK], p[:, _HK:2 * _HK]
    v, g = p[:, 2 * _HK:2 * _HK + _HV], p[:, 2 * _HK + _HV:]

    half = RET_DK // 2
    lane128 = _iota((c, LANES), 1)
    first128 = (lane128 & (RET_DK - 1)) < half

    def rope(a):
        slabs = []
        for j in range(_HK // LANES):
            s = a[:, j * LANES:(j + 1) * LANES]
            slabs.append(jnp.where(first128, pltpu.roll(s, LANES - half, 1), pltpu.roll(s, half, 1)))
        return a * cos_ref[...] + jnp.concatenate(slabs, axis=1) * sin_ref[...]

    q = rope(q)
    k = rope(k) * RET_DK ** -0.5

    lane = _iota((c, _HK), 1)
    head_of_lane = lane >> 6
    lg_lane = jnp.zeros((c, _HK), F32)
    for h in range(RET_HEADS):
        lg_lane = jnp.where(head_of_lane == h, RET_LOG_GAMMA[h], lg_lane)
    row = _iota((c, _HK), 0).astype(F32)
    xi = jnp.exp((row + 1.0) * lg_lane)
    zeta = jnp.exp((c - 1.0 - row) * lg_lane)

    s_prev = s_ref[...]
    cross = _dot(q * xi, s_prev)
    kv = _dot((k * zeta).T, v)

    rel = (_iota((c, c), 0) - _iota((c, c), 1)).astype(F32)
    causal = rel >= 0.0
    relc = jnp.maximum(rel, 0.0)
    for h in range(RET_HEADS):
        qm = jnp.where(head_of_lane == h, q, 0.0)
        sc = _dot_nt(qm, k)
        dec = jnp.where(causal, jnp.exp(relc * RET_LOG_GAMMA[h]), 0.0)
        sl = slice(h * RET_DV, (h + 1) * RET_DV)
        o = _dot(sc * dec, v[:, sl]) + cross[:, sl]
        mu = jnp.mean(o, axis=-1, keepdims=True)
        oc = o - mu
        on = oc * lax.rsqrt(jnp.mean(oc * oc, axis=-1, keepdims=True) + LN_EPS)
        gh = g[:, sl]
        y_ref[0, :, sl] = gh * jax.nn.sigmoid(gh) * on

    rh = _iota(s_ref.shape, 0) >> 6
    ch = _iota(s_ref.shape, 1) >> 7
    chunk_decay = jnp.zeros(s_ref.shape, F32)
    for h in range(RET_HEADS):
        chunk_decay = jnp.where(rh == h, math.exp(c * RET_LOG_GAMMA[h]), chunk_decay)
    s_ref[...] = s_prev * chunk_decay + jnp.where(rh == ch, kv, 0.0)


def _retention(x, w, cos, sin):
    b, t, d = x.shape
    c = min(SEQ_TILE, t)
    return pl.pallas_call(
        _ret_body,
        out_shape=jax.ShapeDtypeStruct((b, t, _HV), F32),
        grid=(b, t // c),
        in_specs=[
            pl.BlockSpec((1, c, d), lambda i, j: (i, j, 0)),
            pl.BlockSpec(w.shape, lambda i, j: (0, 0)),
            pl.BlockSpec((c, _HK), lambda i, j: (j, 0)),
            pl.BlockSpec((c, _HK), lambda i, j: (j, 0)),
        ],
        out_specs=pl.BlockSpec((1, c, _HV), lambda i, j: (i, j, 0)),
        scratch_shapes=[pltpu.VMEM((_HK, _HV), F32)],
        compiler_params=_cparams(("parallel", "arbitrary")),
        name="retention",
    )(x, w, cos, sin)


def _pool_body(x_ref, w_ref, pw_ref, ps_ref, y_ref, ue_ref):
    c = x_ref.shape[1]
    t0 = pl.program_id(1)

    @pl.when(t0 == 0)
    def _():
        ue_ref[0:HALO, :] = jnp.zeros((HALO, ue_ref.shape[1]), F32)

    u = _dot(x_ref[0], w_ref[...])
    ue_ref[HALO:HALO + c, :] = u
    pos = (t0 * c + _iota((c, POOL_DIM), 0)).astype(F32)
    for gi, win in enumerate(POOL_WINDOWS):
        sl = slice(gi * POOL_DIM, (gi + 1) * POOL_DIM)
        s = u[:, sl]
        for sh in range(1, win):
            s = s + ue_ref[HALO - sh:HALO - sh + c, sl]
        cnt = jnp.minimum(pos + 1.0, float(win))
        pooled = s / cnt - u[:, sl]
        y_ref[0, :, sl] = _dot(pooled, pw_ref[gi]) * ps_ref[:, sl]
    ue_ref[0:HALO, :] = ue_ref[c:c + HALO, :]


def _pool(x, w, pool_w, pool_scale):
    b, t, d = x.shape
    c = min(SEQ_TILE, t)
    n = POOL_GROUPS * POOL_DIM
    return pl.pallas_call(
        _pool_body,
        out_shape=jax.ShapeDtypeStruct((b, t, n), F32),
        grid=(b, t // c),
        in_specs=[
            pl.BlockSpec((1, c, d), lambda i, j: (i, j, 0)),
            pl.BlockSpec(w.shape, lambda i, j: (0, 0)),
            pl.BlockSpec(pool_w.shape, lambda i, j: (0, 0, 0)),
            pl.BlockSpec((1, n), lambda i, j: (0, 0)),
        ],
        out_specs=pl.BlockSpec((1, c, n), lambda i, j: (i, j, 0)),
        scratch_shapes=[pltpu.VMEM((HALO + c, n), F32)],
        compiler_params=_cparams(("parallel", "arbitrary")),
        name="pool",
    )(x, w, pool_w, pool_scale.reshape(1, n))


def _unit_lower_inverse(a, ri, ci):
    c = a.shape[0]
    inv = (ri == ci).astype(F32) - jnp.where((ri >> 1) == (ci >> 1), a, 0.0)
    lb = 1
    while (1 << lb) < c:
        off = ((ri >> (lb + 1)) == (ci >> (lb + 1))) & ((ri >> lb) != (ci >> lb))
        inv = inv - _dot(inv, _dot(jnp.where(off, a, 0.0), inv))
        lb += 1
    return inv


def _gdn_body(x_ref, w_ref, wba_ref, cw_ref, hp_ref, nw_ref, y_ref, s_ref, xe_ref):
    c = x_ref.shape[1]
    nqk = GDN_HEADS * GDN_DK
    nconv = 2 * nqk + GDN_HEADS * GDN_DV

    @pl.when(pl.program_id(1) == 0)
    def _():
        s_ref[...] = jnp.zeros(s_ref.shape, F32)
        xe_ref[0:CONV_HALO, :] = jnp.zeros((CONV_HALO, nconv), F32)

    xb = x_ref[0].astype(BF16)
    p = jnp.dot(xb, w_ref[...], preferred_element_type=F32)
    ba = jnp.dot(xb, wba_ref[...], preferred_element_type=F32)
    z = p[:, nconv:]

    xe_ref[CONV_HALO:CONV_HALO + c, :] = p[:, :nconv]
    conv = p[:, :nconv] * cw_ref[GDN_CONV - 1:GDN_CONV, :]
    for j in range(GDN_CONV - 1):
        sh = GDN_CONV - 1 - j
        conv = conv + xe_ref[CONV_HALO - sh:CONV_HALO - sh + c, :] * cw_ref[j:j + 1, :]
    xe_ref[0:CONV_HALO, :] = xe_ref[c:c + CONV_HALO, :]
    act = conv * jax.nn.sigmoid(conv)

    beta_all = jax.nn.sigmoid(ba)
    pre = ba + hp_ref[1:2, :]
    softplus = jnp.maximum(pre, 0.0) + jnp.log1p(jnp.exp(-jnp.abs(pre)))
    glog = -jnp.exp(hp_ref[0:1, :]) * softplus
    rows = _iota((c, LANES), 0)
    gcum = glog
    sh = 1
    while sh < c:
        gcum = gcum + jnp.where(rows >= sh, pltpu.roll(gcum, sh, 0), 0.0)
        sh *= 2
    gcum_t = gcum.T

    ri = _iota((c, c), 0)
    ci = _iota((c, c), 1)
    incl = ri >= ci
    strict = ri > ci

    for h in range(GDN_HEADS):
        qh = act[:, h * GDN_DK:(h + 1) * GDN_DK]
        kh = act[:, nqk + h * GDN_DK:nqk + (h + 1) * GDN_DK]
        vh = act[:, 2 * nqk + h * GDN_DV:2 * nqk + (h + 1) * GDN_DV]
        qh = qh * lax.rsqrt(jnp.sum(qh * qh, axis=-1, keepdims=True) + RMS_EPS) * GDN_DK ** -0.5
        kh = kh * lax.rsqrt(jnp.sum(kh * kh, axis=-1, keepdims=True) + RMS_EPS)
        beta = beta_all[:, h:h + 1]
        gc = gcum[:, GDN_HEADS + h:GDN_HEADS + h + 1]
        gr = gcum_t[GDN_HEADS + h:GDN_HEADS + h + 1, :]
        diff = gc - gr
        decay = jnp.where(incl, jnp.exp(jnp.where(incl, diff, 0.0)), 0.0)
        kb = kh * beta
        a = jnp.where(strict, _dot_nt(kb, kh) * decay, 0.0)
        inv = _unit_lower_inverse(a, ri, ci)
        rhs = jnp.concatenate([vh * beta, kb * jnp.exp(gc)], axis=1)
        sol = _dot(inv, rhs)
        u, w = sol[:, :GDN_DV], sol[:, GDN_DV:]
        attn = _dot_nt(qh, kh) * decay
        g_last = gc[c - 1:c, :]
        k_dec = kh * jnp.exp(g_last - gc)
        q_dec = qh * jnp.exp(gc)
        s = s_ref[h]
        v_new = u - _dot(w, s)
        o = _dot(q_dec, s) + _dot(attn, v_new)
        s_ref[h] = s * jnp.exp(g_last) + _dot(k_dec.T, v_new)
        o = o * lax.rsqrt(jnp.mean(o * o, axis=-1, keepdims=True) + RMS_EPS) * nw_ref[...]
        zh = z[:, h * GDN_DV:(h + 1) * GDN_DV]
        y_ref[0, :, h * GDN_DV:(h + 1) * GDN_DV] = o * (zh * jax.nn.sigmoid(zh))


def _gdn(x, w, wba, conv_w, head_params, norm_w):
    b, t, d = x.shape
    c = min(SEQ_TILE, t)
    n = GDN_HEADS * GDN_DV
    nconv = conv_w.shape[1]
    return pl.pallas_call(
        _gdn_body,
        out_shape=jax.ShapeDtypeStruct((b, t, n), F32),
        grid=(b, t // c),
        in_specs=[
            pl.BlockSpec((1, c, d), lambda i, j: (i, j, 0)),
            pl.BlockSpec(w.shape, lambda i, j: (0, 0)),
            pl.BlockSpec(wba.shape, lambda i, j: (0, 0)),
            pl.BlockSpec(conv_w.shape, lambda i, j: (0, 0)),
            pl.BlockSpec(head_params.shape, lambda i, j: (0, 0)),
            pl.BlockSpec((1, GDN_DV), lambda i, j: (0, 0)),
        ],
        out_specs=pl.BlockSpec((1, c, n), lambda i, j: (i, j, 0)),
        scratch_shapes=[pltpu.VMEM((GDN_HEADS, GDN_DK, GDN_DV), F32),
                        pltpu.VMEM((CONV_HALO + c, nconv), F32)],
        compiler_params=_cparams(("parallel", "arbitrary")),
        name="gdn",
    )(x, w, wba, conv_w, head_params, norm_w.reshape(1, GDN_DV))


def _merge_body(x_ref, yr_ref, yp_ref, yg_ref, wg_ref, wb_ref, wo_ref, g_ref, b_ref, o_ref):
    x = x_ref[...]
    xb = x.astype(BF16)
    d = x.shape[1]
    mixed = None
    for r, y_ref in enumerate((yr_ref, yp_ref, yg_ref)):
        gate = jax.nn.sigmoid(jnp.dot(xb, wg_ref[:, r * d:(r + 1) * d], preferred_element_type=F32))
        term = gate * _dot(y_ref[...], wb_ref[r])
        mixed = term if mixed is None else mixed + term
    h = ALPHA * x + _dot(mixed, wo_ref[...])
    o_ref[...] = _layer_norm(h, g_ref[...], b_ref[...])


def _merge(x, yr, yp, yg, wgate, wb, wo, g, b, tm):
    n, d = x.shape
    tm = min(tm, n)
    row = lambda i: (i, 0)
    return pl.pallas_call(
        _merge_body,
        out_shape=jax.ShapeDtypeStruct((n, d), F32),
        grid=(n // tm,),
        in_specs=[
            pl.BlockSpec((tm, d), row),
            pl.BlockSpec((tm, BRANCH_W), row),
            pl.BlockSpec((tm, BRANCH_W), row),
            pl.BlockSpec((tm, BRANCH_W), row),
            pl.BlockSpec(wgate.shape, lambda i: (0, 0)),
            pl.BlockSpec(wb.shape, lambda i: (0, 0, 0)),
            pl.BlockSpec(wo.shape, lambda i: (0, 0)),
            pl.BlockSpec((1, d), lambda i: (0, 0)),
            pl.BlockSpec((1, d), lambda i: (0, 0)),
        ],
        out_specs=pl.BlockSpec((tm, d), row),
        compiler_params=_cparams(("parallel",)),
        name="merge",
    )(x, yr, yp, yg, wgate, wb, wo, g.reshape(1, d), b.reshape(1, d))


def _router_body(x_ref, w_ref, b_ref, c_ref):
    x = x_ref[...]
    w = w_ref[...]
    x_hi = x.astype(BF16)
    x_lo = (x - x_hi.astype(F32)).astype(BF16)
    w_hi = w.astype(BF16)
    w_lo = (w - w_hi.astype(F32)).astype(BF16)
    mm = lambda a, bb: jnp.dot(a, bb, preferred_element_type=F32)
    logits = mm(x_hi, w_hi) + (mm(x_lo, w_hi) + mm(x_hi, w_lo)) + b_ref[...]

    tm = x.shape[0]
    neg = -1e30
    lane = _iota((tm, LANES), 1)
    lane_f = lane.astype(F32)
    rmax = lambda a: jnp.max(a, axis=-1, keepdims=True)
    rmin = lambda a: jnp.min(a, axis=-1, keepdims=True)
    rsum = lambda a: jnp.sum(a, axis=-1, keepdims=True)

    is_c = (lane >= N_EXPERTS) & (lane < N_EXPERTS + MOE_GROUPS)
    lc = jnp.where(is_c, logits, neg)
    mc = rmax(lc)
    p_g = 1.0 / rsum(jnp.where(is_c, jnp.exp(lc - mc), 0.0))
    g_sel = rmin(jnp.where(is_c & (lc == mc), lane_f, 1e9)) - float(N_EXPERTS)

    is_f = (lane < N_EXPERTS) & ((lane >> 2).astype(F32) == g_sel)
    lf = jnp.where(is_f, logits, neg)
    m1 = rmax(lf)
    i1 = rmin(jnp.where(is_f & (lf == m1), lane_f, 1e9))
    sf = rsum(jnp.where(is_f, jnp.exp(lf - m1), 0.0))
    rest = is_f & (lane_f != i1)
    lf2 = jnp.where(rest, lf, neg)
    m2 = rmax(lf2)
    i2 = rmin(jnp.where(rest & (lf2 == m2), lane_f, 1e9))
    v1 = 1.0 / sf
    v2 = jnp.exp(m2 - m1) / sf
    tot = v1 + v2
    c_ref[...] = jnp.where(lane_f == i1, p_g * (v1 / tot), jnp.where(lane_f == i2, p_g * (v2 / tot), 0.0))


def _router(x, w, b, tm):
    n, d = x.shape
    tm = min(tm, n)
    return pl.pallas_call(
        _router_body,
        out_shape=jax.ShapeDtypeStruct((n, LANES), F32),
        grid=(n // tm,),
        in_specs=[
            pl.BlockSpec((tm, d), lambda i: (i, 0)),
            pl.BlockSpec(w.shape, lambda i: (0, 0)),
            pl.BlockSpec((1, LANES), lambda i: (0, 0)),
        ],
        out_specs=pl.BlockSpec((tm, LANES), lambda i: (i, 0)),
        compiler_params=_cparams(("parallel",)),
        name="router",
    )(x, w, b)


def _expert_body(x_ref, c_ref, wg_ref, wu_ref, wd_ref, g_ref, b_ref, o_ref, acc_ref, xb_ref):
    e = pl.program_id(1)

    @pl.when(e == 0)
    def _():
        acc_ref[...] = jnp.zeros(acc_ref.shape, F32)
        xb_ref[...] = x_ref[...].astype(BF16)

    xb = xb_ref[...]
    hg = jnp.dot(xb, wg_ref[0], preferred_element_type=F32)
    hu = jnp.dot(xb, wu_ref[0], preferred_element_type=F32)
    y = _dot(hg * jax.nn.sigmoid(hg) * hu, wd_ref[0])
    comb = c_ref[...]
    ce = jnp.sum(jnp.where(_iota(comb.shape, 1) == e, comb, 0.0), axis=-1, keepdims=True)
    acc_ref[...] += ce * y

    @pl.when(e == pl.num_programs(1) - 1)
    def _():
        o_ref[...] = _layer_norm(ALPHA * x_ref[...] + acc_ref[...], g_ref[...], b_ref[...])


def _experts(x, comb, wg, wu, wd, g, b, tm):
    n, d = x.shape
    tm = min(tm, n)
    ne, _, hid = wg.shape
    return pl.pallas_call(
        _expert_body,
        out_shape=jax.ShapeDtypeStruct((n, d), F32),
        grid=(n // tm, ne),
        in_specs=[
            pl.BlockSpec((tm, d), lambda i, e: (i, 0)),
            pl.BlockSpec((tm, LANES), lambda i, e: (i, 0)),
            pl.BlockSpec((1, d, hid), lambda i, e: (e, 0, 0)),
            pl.BlockSpec((1, d, hid), lambda i, e: (e, 0, 0)),
            pl.BlockSpec((1, hid, d), lambda i, e: (e, 0, 0)),
            pl.BlockSpec((1, d), lambda i, e: (0, 0)),
            pl.BlockSpec((1, d), lambda i, e: (0, 0)),
        ],
        out_specs=pl.BlockSpec((tm, d), lambda i, e: (i, 0)),
        scratch_shapes=[pltpu.VMEM((tm, d), F32), pltpu.VMEM((tm, d), BF16)],
        compiler_params=_cparams(("parallel", "arbitrary")),
        name="experts",
    )(x, comb, wg, wu, wd, g.reshape(1, d), b.reshape(1, d))


def _rope_tables(t):
    half = RET_DK // 2
    inv_freq = ROPE_BASE ** (-jnp.arange(half, dtype=F32) / half)
    ang = jnp.arange(t, dtype=F32)[:, None] * inv_freq[None, :]
    reps = _HK // half
    cos = jnp.tile(jnp.cos(ang), (1, reps))
    sign = jnp.where((jnp.arange(_HK) % RET_DK) < half, -1.0, 1.0).astype(F32)
    sin = jnp.tile(jnp.sin(ang), (1, reps)) * sign[None, :]
    return cos, sin


def _pad_lanes(a):
    return jnp.pad(a, ((0, 0), (0, LANES - a.shape[1])))


def kernel(x, w_in, pool_w, pool_scale, conv_w, A_log, dt_bias, gdn_norm_w, w_branch, w_out, ln1_g, ln1_b,
           router_coarse_w, router_coarse_b, router_fine_w, router_fine_b, w_gate, w_up, w_down, ln2_g, ln2_b):
    b, t, d = x.shape
    n = b * t
    cos, sin = _rope_tables(t)
    for l in range(DEPTH):
        wl = w_in[l]
        w_ret = wl[:, RET_COLS[0]:RET_COLS[1]].astype(BF16)
        w_pool = wl[:, POOL_COLS[0]:POOL_COLS[1]].astype(BF16)
        w_gdn = wl[:, GDN_COLS[0]:GDN_COLS[1]].astype(BF16)
        w_ba = _pad_lanes(wl[:, BA_COLS[0]:BA_COLS[1]]).astype(BF16)
        w_gates = wl[:, GATE_COLS[0]:GATE_COLS[1]].astype(BF16)
        pad4 = lambda a: jnp.pad(a, (GDN_HEADS, LANES - 2 * GDN_HEADS))
        head_params = jnp.stack([pad4(A_log[l]), pad4(dt_bias[l])] + [jnp.zeros((LANES,), F32)] * 6)

        y_ret = _retention(x, w_ret, cos, sin)
        y_pool = _pool(x, w_pool, pool_w[l].astype(BF16), pool_scale[l])
        y_gdn = _gdn(x, w_gdn, w_ba, conv_w[l], head_params, gdn_norm_w[l])
        x1 = _merge(x.reshape(n, d), y_ret.reshape(n, BRANCH_W), y_pool.reshape(n, BRANCH_W),
                    y_gdn.reshape(n, BRANCH_W), w_gates, w_branch[l].astype(BF16), w_out[l].astype(BF16),
                    ln1_g[l], ln1_b[l], tm=512)

        x2 = _moe_layer(x1, router_coarse_w[l], router_coarse_b[l], router_fine_w[l], router_fine_b[l],
                        w_gate[l], w_up[l], w_down[l], ln2_g[l], ln2_b[l])
        x = x2.reshape(b, t, d)
    return x


def _moe_layer(x1, rcw, rcb, rfw, rfb, wg, wu, wd, g, b):
    w_router = _pad_lanes(jnp.concatenate([rfw, rcw], axis=1))
    b_router = _pad_lanes(jnp.concatenate([rfb, rcb])[None, :])
    comb = _router(x1, w_router, b_router, tm=512)
    return _experts(x1, comb, wg.astype(BF16), wu.astype(BF16), wd.astype(BF16), g, b, tm=1024)
```
